```python
import math
import jax, jax.numpy as jnp
from jax import lax
import numpy as np

D_MODEL = 2048
BATCH = 4
SEQ = 4096
DEPTH = 1

D_RNN = D_MODEL
RNN_BLOCKS = 16
RNN_BLOCK = D_RNN // RNN_BLOCKS
RNN_CONV = 4
LRU_C = 8.0
HEAD_DIM = 128
N_HEADS = D_MODEL // HEAD_DIM
N_KV_HEADS = 4
GROUP = N_HEADS // N_KV_HEADS
IDX_HEADS = 16
IDX_DIM = 64
TOPK_MAX = 256
Q_BLOCK = 128
ROPE_THETA = 500000.0
ROPE_FRAC = 4
D_FF = 256 * ((8 * D_MODEL // 3 + 255) // 256)
FFN_CONV = 3
LN_EPS = 1e-5
DEEPNORM_ALPHA = (2 * DEPTH) ** 0.25
DEEPNORM_BETA = (8 * DEPTH) ** -0.25

SPLITS = [D_RNN,
          D_RNN,
          N_HEADS * HEAD_DIM,
          N_KV_HEADS * HEAD_DIM,
          N_KV_HEADS * HEAD_DIM,
          IDX_HEADS * IDX_DIM,
          IDX_DIM,
          IDX_HEADS,
          D_MODEL,
          D_MODEL]
D_IN = sum(SPLITS)
SPLIT_POINTS = [int(v) for v in np.cumsum(SPLITS)[:-1]]

kernel_name = 'hybrid_rglru_dsa_convffn_deepnorm'


def layer_norm(x, g, b):
    xf = x.astype(jnp.float32)
    mu = jnp.mean(xf, axis=-1, keepdims=True)
    var = jnp.mean(jnp.square(xf - mu), axis=-1, keepdims=True)
    y = (xf - mu) * lax.rsqrt(var + LN_EPS)
    return (y * g.astype(jnp.float32) + b.astype(jnp.float32)).astype(x.dtype)


def rope_partial(x, pos):
    d = x.shape[-1]
    rd = d // ROPE_FRAC
    half = rd // 2
    inv = jnp.power(ROPE_THETA, -jnp.arange(half, dtype=jnp.float32) * 2.0 / rd)
    ang = pos.astype(jnp.float32)[:, None] * inv[None, :]
    cos = jnp.cos(ang)[:, None, :]
    sin = jnp.sin(ang)[:, None, :]
    xf = x.astype(jnp.float32)
    x1 = xf[..., :half]
    x2 = xf[..., half:rd]
    out = jnp.concatenate([x1 * cos - x2 * sin, x2 * cos + x1 * sin, xf[..., rd:]], axis=-1)
    return out.astype(x.dtype)


def causal_dwconv(x, w, b):
    width = w.shape[0]
    y = lax.conv_general_dilated(
        x, w[:, None, :].astype(x.dtype), window_strides=(1,), padding=[(width - 1, 0)],
        dimension_numbers=('NWC', 'WIO', 'NWC'), feature_group_count=x.shape[-1])
    return y + b


def rg_lru(x, w_a, b_a, w_i, b_i, lam):
    B, T, _ = x.shape
    xb = x.reshape(B, T, RNN_BLOCKS, RNN_BLOCK)
    r = jax.nn.sigmoid(jnp.einsum('btnc,ncd->btnd', xb, w_a) + b_a).reshape(B, T, D_RNN)
    i = jax.nn.sigmoid(jnp.einsum('btnc,ncd->btnd', xb, w_i) + b_i).reshape(B, T, D_RNN)
    log_a = -LRU_C * r.astype(jnp.float32) * jax.nn.softplus(-lam.astype(jnp.float32))
    a = jnp.exp(log_a)
    mult = jnp.sqrt(-jnp.expm1(2.0 * log_a))
    u = mult * (i * x).astype(jnp.float32)

    def combine(left, right):
        a1, b1 = left
        a2, b2 = right
        return a1 * a2, a2 * b1 + b2

    _, h = lax.associative_scan(combine, (a, u), axis=1)
    return h.astype(x.dtype)


def dsa_attention(q, k, v, qi, ki, wi, topk):
    B, T = q.shape[0], q.shape[1]
    nblk = T // Q_BLOCK
    kif = ki.astype(jnp.float32)
    key_pos = jnp.arange(T)

    def to_blocks(a):
        return a.reshape((B, nblk, Q_BLOCK) + a.shape[2:]).swapaxes(0, 1)

    def one_block(args):
        blk, qb, qib, wib = args
        qpos = blk * Q_BLOCK + jnp.arange(Q_BLOCK)
        causal = key_pos[None, :] <= qpos[:, None]
        logits = jnp.einsum('bqhd,bsd->bqhs', qib.astype(jnp.float32), kif) * (IDX_DIM ** -0.5)
        score = jnp.einsum('bqh,bqhs->bqs', wib.astype(jnp.float32), jax.nn.relu(logits))
        score = jnp.where(causal[None], score, -jnp.inf)
        _, idx = lax.top_k(score, topk)
        valid = idx <= qpos[None, :, None]
        kg = jax.vmap(lambda kk, ii: kk[ii])(k, idx)
        vg = jax.vmap(lambda vv, ii: vv[ii])(v, idx)
        qg = qb.reshape(B, Q_BLOCK, N_KV_HEADS, GROUP, HEAD_DIM)
        s = jnp.einsum('bqngd,bqknd->bqngk', qg, kg).astype(jnp.float32) * (HEAD_DIM ** -0.5)
        s = jnp.where(valid[:, :, None, None, :], s, -jnp.inf)
        p = jax.nn.softmax(s, axis=-1).astype(v.dtype)
        o = jnp.einsum('bqngk,bqknd->bqngd', p, vg)
        return o.reshape(B, Q_BLOCK, N_HEADS * HEAD_DIM)

    out = lax.map(one_block, (jnp.arange(nblk), to_blocks(q), to_blocks(qi), to_blocks(wi)))
    return out.swapaxes(0, 1).reshape(B, T, N_HEADS * HEAD_DIM)


def hybrid_layer(x, pos, topk, w_in, rnn_conv_w, rnn_conv_b, lru_wa, lru_ba, lru_wi, lru_bi,
                 lru_lambda, w_out, ln1_g, ln1_b, w_up, ffn_conv_w, ffn_conv_b, w_down, ln2_g, ln2_b):
    B, T, _ = x.shape
    h = jnp.einsum('btd,de->bte', x, w_in)
    xr, gr, q, k, v, qi, ki, wi, g_rnn, g_att = jnp.split(h, SPLIT_POINTS, axis=-1)
    xr = causal_dwconv(xr, rnn_conv_w, rnn_conv_b)
    y_rnn = rg_lru(xr, lru_wa, lru_ba, lru_wi, lru_bi, lru_lambda) * jax.nn.gelu(gr)
    q = rope_partial(q.reshape(B, T, N_HEADS, HEAD_DIM), pos)
    k = rope_partial(k.reshape(B, T, N_KV_HEADS, HEAD_DIM), pos)
    v = v.reshape(B, T, N_KV_HEADS, HEAD_DIM)
    qi = rope_partial(qi.reshape(B, T, IDX_HEADS, IDX_DIM), pos)
    ki = rope_partial(ki.reshape(B, T, 1, IDX_DIM), pos)[:, :, 0, :]
    wi = wi * (IDX_HEADS ** -0.5)
    y_att = dsa_attention(q, k, v, qi, ki, wi, topk)
    merged = jax.nn.sigmoid(g_rnn) * y_rnn + jax.nn.sigmoid(g_att) * y_att
    x = layer_norm(DEEPNORM_ALPHA * x + jnp.einsum('btd,de->bte', merged, w_out), ln1_g, ln1_b)
    u = causal_dwconv(jnp.einsum('btd,df->btf', x, w_up), ffn_conv_w, ffn_conv_b)
    gate, up = jnp.split(u, [D_FF], axis=-1)
    f = jnp.einsum('btf,fd->btd', jax.nn.silu(gate) * up, w_down)
    x = layer_norm(DEEPNORM_ALPHA * x + f, ln2_g, ln2_b)
    return x


def setup_inputs(seed: int = 0) -> dict:
    key = jax.random.key(seed)
    ks = jax.random.split(key, 20)
    nrm = jax.random.normal
    f32 = jnp.float32
    x = nrm(ks[0], (BATCH, SEQ, D_MODEL), f32)
    w_in = nrm(ks[1], (DEPTH, D_MODEL, D_IN), f32) * (D_MODEL ** -0.5)
    rnn_conv_w = nrm(ks[2], (DEPTH, RNN_CONV, D_RNN), f32) * (RNN_CONV ** -0.5)
    rnn_conv_b = 0.01 * nrm(ks[3], (DEPTH, D_RNN), f32)
    lru_wa = nrm(ks[4], (DEPTH, RNN_BLOCKS, RNN_BLOCK, RNN_BLOCK), f32) * (RNN_BLOCK ** -0.5)
    lru_ba = 0.01 * nrm(ks[5], (DEPTH, RNN_BLOCKS, RNN_BLOCK), f32)
    lru_wi = nrm(ks[6], (DEPTH, RNN_BLOCKS, RNN_BLOCK, RNN_BLOCK), f32) * (RNN_BLOCK ** -0.5)
    lru_bi = 0.01 * nrm(ks[7], (DEPTH, RNN_BLOCKS, RNN_BLOCK), f32)
    ac = jax.random.uniform(ks[8], (DEPTH, D_RNN), f32, minval=0.9, maxval=0.999)
    a0 = jnp.power(ac, 1.0 / LRU_C)
    lru_lambda = jnp.log(a0) - jnp.log1p(-a0)
    w_out = nrm(ks[9], (DEPTH, D_MODEL, D_MODEL), f32) * (D_MODEL ** -0.5) * DEEPNORM_BETA
    ln1_g = 1.0 + 0.01 * nrm(ks[10], (DEPTH, D_MODEL), f32)
    ln1_b = 0.01 * nrm(ks[11], (DEPTH, D_MODEL), f32)
    w_up = nrm(ks[12], (DEPTH, D_MODEL, 2 * D_FF), f32) * (D_MODEL ** -0.5)
    ffn_conv_w = nrm(ks[13], (DEPTH, FFN_CONV, 2 * D_FF), f32) * (FFN_CONV ** -0.5)
    ffn_conv_b = 0.01 * nrm(ks[14], (DEPTH, 2 * D_FF), f32)
    w_down = nrm(ks[15], (DEPTH, D_FF, D_MODEL), f32) * (D_FF ** -0.5) * DEEPNORM_BETA
    ln2_g = 1.0 + 0.01 * nrm(ks[16], (DEPTH, D_MODEL), f32)
    ln2_b = 0.01 * nrm(ks[17], (DEPTH, D_MODEL), f32)
    return {'x': x, 'w_in': w_in, 'rnn_conv_w': rnn_conv_w, 'rnn_conv_b': rnn_conv_b,
            'lru_wa': lru_wa, 'lru_ba': lru_ba, 'lru_wi': lru_wi, 'lru_bi': lru_bi,
            'lru_lambda': lru_lambda, 'w_out': w_out, 'ln1_g': ln1_g, 'ln1_b': ln1_b,
            'w_up': w_up, 'ffn_conv_w': ffn_conv_w, 'ffn_conv_b': ffn_conv_b, 'w_down': w_down,
            'ln2_g': ln2_g, 'ln2_b': ln2_b}


def reference(x, w_in, rnn_conv_w, rnn_conv_b, lru_wa, lru_ba, lru_wi, lru_bi, lru_lambda,
              w_out, ln1_g, ln1_b, w_up, ffn_conv_w, ffn_conv_b, w_down, ln2_g, ln2_b):
    T = x.shape[1]
    topk = min(TOPK_MAX, T // 4)
    pos = jnp.arange(T, dtype=jnp.int32)
    for l in range(DEPTH):
        x = hybrid_layer(x, pos, topk, w_in[l], rnn_conv_w[l], rnn_conv_b[l], lru_wa[l], lru_ba[l],
                         lru_wi[l], lru_bi[l], lru_lambda[l], w_out[l], ln1_g[l], ln1_b[l],
                         w_up[l], ffn_conv_w[l], ffn_conv_b[l], w_down[l], ln2_g[l], ln2_b[l])
    return x
```

```python
import functools
import math

import numpy as np
import jax
import jax.numpy as jnp
from jax import lax
from jax.experimental import pallas as pl
from jax.experimental.pallas import tpu as pltpu

F32 = jnp.float32
BF16 = jnp.bfloat16

D_MODEL = 2048
D_RNN = D_MODEL
RNN_BLOCKS = 16
RNN_BLOCK = D_RNN // RNN_BLOCKS
RNN_CONV = 4
LRU_C = 8.0
HEAD_DIM = 128
N_HEADS = D_MODEL // HEAD_DIM
N_KV_HEADS = 4
GROUP = N_HEADS // N_KV_HEADS
IDX_HEADS = 16
IDX_DIM = 64
TOPK_MAX = 256
ROPE_THETA = 500000.0
ROPE_FRAC = 4
D_FF = 256 * ((8 * D_MODEL // 3 + 255) // 256)
FFN_CONV = 3
LN_EPS = 1e-5

_SPLITS = (D_RNN, D_RNN, N_HEADS * HEAD_DIM, N_KV_HEADS * HEAD_DIM, N_KV_HEADS * HEAD_DIM,
           IDX_HEADS * IDX_DIM, IDX_DIM, IDX_HEADS, D_MODEL, D_MODEL)
_OFF = tuple(int(v) for v in np.cumsum((0,) + _SPLITS))
(_O_XR, _O_GR, _O_Q, _O_K, _O_V, _O_QI, _O_KI, _O_WI, _O_GRNN, _O_GATT, _O_END) = _OFF

LANES = 128
SUBLANES = 8
BF16_ROWS = 16
VMEM_LIMIT = 56 * 1024 * 1024

BISECT_ITERS = 32
KEY_CHUNK = 512
Q_TILE = 128
RNN_CHUNK = 256


def _cparams(sem):
    return pltpu.CompilerParams(dimension_semantics=sem, vmem_limit_bytes=VMEM_LIMIT)


def _proj_body(*refs, act, shift, scale):
    if act == "rope":
        x_ref, w_ref, tab_ref, o_ref = refs
    else:
        x_ref, w_ref, o_ref = refs
    acc = jnp.dot(x_ref[...], w_ref[...], preferred_element_type=F32)
    if act == "none":
        o_ref[...] = acc.astype(o_ref.dtype)
    elif act == "scale":
        o_ref[...] = (acc * scale).astype(o_ref.dtype)
    elif act == "gelu":
        o_ref[...] = jax.nn.gelu(acc, approximate=True).astype(o_ref.dtype)
    elif act == "sigmoid":
        o_ref[...] = jax.nn.sigmoid(acc).astype(o_ref.dtype)
    else:
        c, sa, sb = tab_ref[0], tab_ref[1], tab_ref[2]
        for p in range(acc.shape[1] // LANES):
            y = acc[:, p * LANES:(p + 1) * LANES]
            lo = pltpu.roll(y, shift, 1)
            hi = pltpu.roll(y, LANES - shift, 1)
            o_ref[:, p * LANES:(p + 1) * LANES] = (y * c + lo * sa + hi * sb).astype(o_ref.dtype)


def _proj(xb, w, *, act, out_dtype, tm, tn, name, tab=None, kind_bounds=(), shift=0, seq_len=None,
          scale=1.0):
    n, d = xb.shape
    ncol = w.shape[1]
    assert n % tm == 0 and ncol % tn == 0
    in_specs = [pl.BlockSpec((tm, d), lambda i, j: (i, 0)),
                pl.BlockSpec((d, tn), lambda i, j: (0, j))]
    args = [xb, w]
    if act == "rope":
        assert seq_len % tm == 0
        tiles_per_seq = seq_len // tm

        def tab_map(i, j):
            kind = 0
            for b in kind_bounds:
                kind = kind + (j >= b).astype(jnp.int32)
            return (kind, 0, i % tiles_per_seq, 0)

        in_specs.append(pl.BlockSpec((None, 3, tm, LANES), tab_map))
        args.append(tab)
    return pl.pallas_call(
        functools.partial(_proj_body, act=act, shift=shift, scale=scale),
        grid=(n // tm, ncol // tn),
        in_specs=in_specs,
        out_specs=pl.BlockSpec((tm, tn), lambda i, j: (i, j)),
        out_shape=jax.ShapeDtypeStruct((n, ncol), out_dtype),
        compiler_params=_cparams(("parallel", "arbitrary")),
        name=name,
    )(*args)


def _rope_tables(seq_len, head_dim, scale):
    rd = head_dim // ROPE_FRAC
    half = rd // 2
    inv = jnp.power(ROPE_THETA, -jnp.arange(half, dtype=F32) * 2.0 / rd)
    ang = jnp.arange(seq_len, dtype=F32)[:, None] * inv[None, :]
    cos, sin = jnp.cos(ang), jnp.sin(ang)
    rest = head_dim - rd
    one = jnp.ones((seq_len, rest), F32)
    zero_h = jnp.zeros((seq_len, half), F32)
    zero_r = jnp.zeros((seq_len, rest), F32)
    c = jnp.concatenate([cos, cos, one], axis=1)
    sa = jnp.concatenate([zero_h, sin, zero_r], axis=1)
    sb = jnp.concatenate([-sin, zero_h, zero_r], axis=1)
    reps = LANES // head_dim
    tab = jnp.stack([jnp.tile(t, (1, reps)) for t in (c, sa, sb)], axis=0)
    return tab * scale


def _identity_tables(seq_len):
    one = jnp.ones((seq_len, LANES), F32)
    zero = jnp.zeros((seq_len, LANES), F32)
    return jnp.stack([one, zero, zero], axis=0)


EXPM1_SERIES_BELOW = 0.125


def _neg_expm1(y):
    series = 1.0 + y * (1.0 / 6.0)
    for k in (5.0, 4.0, 3.0, 2.0):
        series = 1.0 + (y * (1.0 / k)) * series
    return jnp.where(y > -EXPM1_SERIES_BELOW, -y * series, 1.0 - jnp.exp(y))


def _rnn_body(xr_ref, gg_ref, sg_ref, cw_ref, cb_ref, wa_ref, ba_ref, wi_ref, bi_ref, lam_ref, o_ref, *, tt):
    seq_len = xr_ref.shape[0]
    lam = lam_ref[...]
    softplus_neg_lam = jnp.maximum(-lam, 0.0) + jnp.log1p(jnp.exp(-jnp.abs(lam)))
    cw = cw_ref[...]
    cb = cb_ref[...]
    wa = wa_ref[...].astype(BF16)
    wi = wi_ref[...].astype(BF16)
    ba = ba_ref[...]
    bi = bi_ref[...]
    row_in_group = lax.broadcasted_iota(jnp.int32, (tt, RNN_BLOCK), 0) & (SUBLANES - 1)

    def chunk(ci, h_prev):
        t0 = pl.multiple_of(ci * tt, tt)
        cur = xr_ref[pl.ds(t0, tt), :]
        prev = xr_ref[pl.ds(jnp.maximum(t0 - SUBLANES, 0), SUBLANES), :]
        prev = jnp.where(ci > 0, prev, 0.0)
        ext = jnp.concatenate([prev, cur], axis=0)
        xc = cw[RNN_CONV - 1:RNN_CONV] * cur + cb
        for back in range(1, RNN_CONV):
            shifted = pltpu.roll(ext, back, 0)[SUBLANES:]
            xc = xc + cw[RNN_CONV - 1 - back:RNN_CONV - back] * shifted
        xcb = xc.astype(BF16)
        r = jax.nn.sigmoid(jnp.dot(xcb, wa, preferred_element_type=F32) + ba)
        i = jax.nn.sigmoid(jnp.dot(xcb, wi, preferred_element_type=F32) + bi)
        log_a = (-LRU_C * r) * softplus_neg_lam
        a = jnp.exp(log_a)
        mult = jnp.sqrt(_neg_expm1(2.0 * log_a))
        u = mult * (i * xc)
        for d in (1, 2, 4):
            a_s = pltpu.roll(a, d, 0)
            u_s = pltpu.roll(u, d, 0)
            take = row_in_group >= d
            u = jnp.where(take, a * u_s + u, u)
            a = jnp.where(take, a * a_s, a)
        h = h_prev
        for g in range(tt // SUBLANES):
            rows = slice(g * SUBLANES, (g + 1) * SUBLANES)
            hg = a[rows] * h + u[rows]
            dst = pl.ds(t0 + g * SUBLANES, SUBLANES)
            o_ref[dst, :] = hg * gg_ref[dst, :] * sg_ref[dst, :]
            h = hg[SUBLANES - 1:SUBLANES]
        return h

    lax.fori_loop(0, seq_len // tt, chunk, jnp.zeros((1, RNN_BLOCK), F32))


def _rnn(xr, gg, gates, cw, cb, wa, ba, wi, bi, lam, *, tt):
    bsz, seq_len, _ = xr.shape
    cblk = RNN_BLOCK
    seq_spec = lambda: pl.BlockSpec((None, seq_len, cblk), lambda b, c: (b, 0, c))
    vec_spec = lambda rows: pl.BlockSpec((rows, cblk), lambda b, c: (0, c))
    return pl.pallas_call(
        functools.partial(_rnn_body, tt=tt),
        grid=(bsz, D_RNN // cblk),
        in_specs=[seq_spec(), seq_spec(), seq_spec(),
                  vec_spec(RNN_CONV), vec_spec(1),
                  pl.BlockSpec((None, RNN_BLOCK, RNN_BLOCK), lambda b, c: (c, 0, 0)),
                  pl.BlockSpec((None, 1, RNN_BLOCK), lambda b, c: (c, 0, 0)),
                  pl.BlockSpec((None, RNN_BLOCK, RNN_BLOCK), lambda b, c: (c, 0, 0)),
                  pl.BlockSpec((None, 1, RNN_BLOCK), lambda b, c: (c, 0, 0)),
                  vec_spec(1)],
        out_specs=seq_spec(),
        out_shape=jax.ShapeDtypeStruct((bsz, seq_len, D_RNN), F32),
        compiler_params=_cparams(("parallel", "parallel")),
        name="rnn",
    )(xr, gg, gates, cw, cb, wa, ba.reshape(RNN_BLOCKS, 1, RNN_BLOCK), wi, bi.reshape(RNN_BLOCKS, 1, RNN_BLOCK),
      lam)


def _nt_dot(a, b):
    return lax.dot_general(a, b, (((1,), (1,)), ((), ())), preferred_element_type=F32)


def _attn_body(qi_ref, kab_ref, wi_ref, q_ref, k_ref, v_ref, yr_ref, sg_ref, o_ref,
               sc_ref, sct_ref, wb_ref, m_ref, l_ref, acc_ref, *, topk, kc):
    tq = qi_ref.shape[0]
    i = pl.program_id(1)
    q0 = i * tq
    n_chunks = (q0 + tq + kc - 1) // kc
    neg_inf = -jnp.inf

    for h in range(IDX_HEADS):
        wb_ref[h] = jnp.broadcast_to(wi_ref[:, h:h + 1], (tq, LANES))
    pairs = IDX_HEADS // 2
    qs = jnp.concatenate([qi_ref[:, j * LANES:(j + 1) * LANES] for j in range(pairs)], axis=0)
    qpos = q0 + lax.broadcasted_iota(jnp.int32, (tq, LANES), 0)
    lane = lax.broadcasted_iota(jnp.int32, (tq, LANES), 1)

    def score_chunk(c, carry):
        k0 = pl.multiple_of(c * kc, kc)
        kab = kab_ref[pl.ds(k0, kc), :]
        both = jnp.concatenate([kab[:, :LANES], kab[:, LANES:]], axis=0)
        logits = _nt_dot(qs, both)
        for cb in range(kc // LANES):
            acc = jnp.zeros((tq, LANES), F32)
            for h in range(IDX_HEADS):
                j, e = h // 2, h % 2
                blk = logits[j * tq:(j + 1) * tq, e * kc + cb * LANES:e * kc + (cb + 1) * LANES]
                acc = acc + wb_ref[h] * jnp.maximum(blk, 0.0)
            kpos = k0 + cb * LANES + lane
            acc = jnp.where(kpos <= qpos, acc, neg_inf)
            sc_ref[:, pl.ds(k0 + cb * LANES, LANES)] = acc
            sct_ref[pl.ds(k0 + cb * LANES, LANES), :] = acc.T
        return carry

    lax.fori_loop(0, n_chunks, score_chunk, 0)

    def minmax(c, carry):
        mn, mx = carry
        blk = sct_ref[pl.ds(pl.multiple_of(c * kc, kc), kc), :]
        blk3 = blk.reshape(kc // SUBLANES, SUBLANES, tq)
        mx = jnp.maximum(mx, jnp.max(blk3, axis=0))
        mn = jnp.minimum(mn, jnp.min(jnp.where(blk3 == neg_inf, jnp.inf, blk3), axis=0))
        return mn, mx

    mn8, mx8 = lax.fori_loop(0, n_chunks, minmax,
                             (jnp.full((SUBLANES, tq), jnp.inf, F32), jnp.full((SUBLANES, tq), neg_inf, F32)))
    row_min = jnp.min(mn8, axis=0, keepdims=True)
    row_max = jnp.max(mx8, axis=0, keepdims=True)
    kf = float(topk)

    def bisect(_, carry):
        lo, hi = carry
        mid = 0.5 * lo + 0.5 * hi

        def count(c, part):
            blk = sct_ref[pl.ds(pl.multiple_of(c * kc, kc), kc), :]
            ge = jnp.where(blk >= mid, 1.0, 0.0).reshape(kc // SUBLANES, SUBLANES, tq)
            return part + jnp.sum(ge, axis=0)

        part = lax.fori_loop(0, n_chunks, count, jnp.zeros((SUBLANES, tq), F32))
        enough = jnp.sum(part, axis=0, keepdims=True) >= kf
        return jnp.where(enough, mid, lo), jnp.where(enough, hi, mid)

    lo, _ = lax.fori_loop(0, BISECT_ITERS, bisect, (row_min, row_max))
    n_causal = (q0 + 1 + lax.broadcasted_iota(jnp.int32, (1, tq), 1)).astype(F32)
    thr_row = jnp.where(n_causal <= kf, row_min, lo)
    thr_col = jnp.broadcast_to(thr_row, (tq, tq)).T[:, 0:1]

    m_ref[...] = jnp.full(m_ref.shape, neg_inf, F32)
    l_ref[...] = jnp.zeros(l_ref.shape, F32)
    acc_ref[...] = jnp.zeros(acc_ref.shape, F32)
    qstack = [jnp.concatenate([q_ref[:, (n * GROUP + g) * HEAD_DIM:(n * GROUP + g + 1) * HEAD_DIM]
                               for g in range(GROUP)], axis=0) for n in range(N_KV_HEADS)]

    def attend(c, carry):
        k0 = pl.multiple_of(c * kc, kc)
        keep = sc_ref[:, pl.ds(k0, kc)] >= thr_col
        bias = jnp.where(keep, 0.0, neg_inf)
        bias = jnp.concatenate([bias] * GROUP, axis=0)
        for n in range(N_KV_HEADS):
            kn = k_ref[pl.ds(k0, kc), n * HEAD_DIM:(n + 1) * HEAD_DIM]
            vn = v_ref[pl.ds(k0, kc), n * HEAD_DIM:(n + 1) * HEAD_DIM]
            s = _nt_dot(qstack[n], kn) + bias
            m_old = m_ref[n]
            m_new = jnp.maximum(m_old, jnp.max(s, axis=1, keepdims=True))
            m_fin = jnp.where(m_new == neg_inf, 0.0, m_new)
            alpha = jnp.exp(m_old - m_fin)
            p = jnp.exp(s - m_fin)
            l_ref[n] = alpha * l_ref[n] + jnp.sum(p, axis=1, keepdims=True)
            acc_ref[n] = alpha * acc_ref[n] + jnp.dot(p.astype(BF16), vn, preferred_element_type=F32)
            m_ref[n] = m_new
        return carry

    lax.fori_loop(0, n_chunks, attend, 0)

    for n in range(N_KV_HEADS):
        y = acc_ref[n] / l_ref[n]
        for g in range(GROUP):
            cols = slice((n * GROUP + g) * HEAD_DIM, (n * GROUP + g + 1) * HEAD_DIM)
            yh = y[g * tq:(g + 1) * tq]
            o_ref[:, cols] = (yr_ref[:, cols] + sg_ref[:, cols] * yh).astype(o_ref.dtype)


def _attention(qi, kab, wi, qkv, yr, gates, *, topk, tq, kc):
    bsz, seq_len, _ = qi.shape
    dq = N_HEADS * HEAD_DIM
    dkv = N_KV_HEADS * HEAD_DIM
    tile = lambda width, col: pl.BlockSpec((None, tq, width), lambda b, i: (b, i, col))
    whole = lambda width, col: pl.BlockSpec((None, seq_len, width), lambda b, i: (b, 0, col))
    return pl.pallas_call(
        functools.partial(_attn_body, topk=topk, kc=kc),
        grid=(bsz, seq_len // tq),
        in_specs=[tile(IDX_HEADS * IDX_DIM, 0),
                  whole(2 * LANES, 0),
                  tile(LANES, 0),
                  tile(dq, 0),
                  whole(dkv, dq // dkv),
                  whole(dkv, dq // dkv + 1),
                  tile(D_MODEL, 0),
                  tile(D_MODEL, 1)],
        out_specs=tile(D_MODEL, 0),
        out_shape=jax.ShapeDtypeStruct((bsz, seq_len, D_MODEL), BF16),
        scratch_shapes=[pltpu.VMEM((tq, seq_len), F32),
                        pltpu.VMEM((seq_len, tq), F32),
                        pltpu.VMEM((IDX_HEADS, tq, LANES), F32),
                        pltpu.VMEM((N_KV_HEADS, GROUP * tq, 1), F32),
                        pltpu.VMEM((N_KV_HEADS, GROUP * tq, 1), F32),
                        pltpu.VMEM((N_KV_HEADS, GROUP * tq, HEAD_DIM), F32)],
        compiler_params=_cparams(("parallel", "arbitrary")),
        name="attention",
    )(qi, kab, wi, qkv, qkv, qkv, yr, gates)


def _layer_norm(z, g, b):
    mu = jnp.mean(z, axis=-1, keepdims=True)
    zc = z - mu
    var = jnp.mean(zc * zc, axis=-1, keepdims=True)
    return zc * lax.rsqrt(var + LN_EPS) * g + b


def _oproj_body(m_ref, w_ref, x_ref, g_ref, b_ref, o_ref, ob_ref, *, alpha):
    y = jnp.dot(m_ref[...], w_ref[...], preferred_element_type=F32)
    out = _layer_norm(alpha * x_ref[...] + y, g_ref[...], b_ref[...])
    o_ref[...] = out
    ob_ref[...] = out.astype(BF16)


def _oproj(merged, w, x, g, b, *, alpha, tm):
    n, d = x.shape
    row = lambda: pl.BlockSpec((tm, d), lambda i: (i, 0))
    vec = lambda: pl.BlockSpec((1, d), lambda i: (0, 0))
    return pl.pallas_call(
        functools.partial(_oproj_body, alpha=alpha),
        grid=(n // tm,),
        in_specs=[row(), pl.BlockSpec((d, d), lambda i: (0, 0)), row(), vec(), vec()],
        out_specs=[row(), row()],
        out_shape=[jax.ShapeDtypeStruct((n, d), F32), jax.ShapeDtypeStruct((n, d), BF16)],
        compiler_params=_cparams(("parallel",)),
        name="out_proj_ln",
    )(merged, w, x, g, b)


def _ffn_up_body(x_ref, halo_ref, wg_ref, wu_ref, cg_ref, cu_ref, bg_ref, bu_ref, o_ref, xcat_ref, *,
                 tiles_per_seq):
    i = pl.program_id(0)
    j = pl.program_id(1)
    tm = x_ref.shape[0]

    @pl.when(j == 0)
    def _():
        starts_seq = (i % tiles_per_seq) == 0
        xcat_ref[0:BF16_ROWS, :] = jnp.where(starts_seq, jnp.zeros_like(halo_ref[...]), halo_ref[...])
        xcat_ref[BF16_ROWS:, :] = x_ref[...]

    xcat = xcat_ref[...]

    def conv(w_ref, c_ref, b_ref):
        u = jnp.dot(xcat, w_ref[...], preferred_element_type=F32)
        c = c_ref[...]
        y = c[FFN_CONV - 1:FFN_CONV] * u[BF16_ROWS:] + b_ref[...]
        for back in range(1, FFN_CONV):
            y = y + c[FFN_CONV - 1 - back:FFN_CONV - back] * pltpu.roll(u, back, 0)[BF16_ROWS:]
        return y

    gate = conv(wg_ref, cg_ref, bg_ref)
    up = conv(wu_ref, cu_ref, bu_ref)
    o_ref[...] = (jax.nn.silu(gate) * up).astype(o_ref.dtype)


def _ffn_up(x1b, w_gate, w_up, conv_w, conv_b, *, seq_len, tm, tn):
    n, d = x1b.shape
    nj = D_FF // tn
    halo_blocks = tm // BF16_ROWS
    col = lambda rows, off: pl.BlockSpec((rows, tn), lambda i, j: (0, j + off))
    return pl.pallas_call(
        functools.partial(_ffn_up_body, tiles_per_seq=seq_len // tm),
        grid=(n // tm, nj),
        in_specs=[pl.BlockSpec((tm, d), lambda i, j: (i, 0)),
                  pl.BlockSpec((BF16_ROWS, d), lambda i, j: (jnp.maximum(i * halo_blocks - 1, 0), 0)),
                  col(d, 0), col(d, nj),
                  col(FFN_CONV, 0), col(FFN_CONV, nj), col(1, 0), col(1, nj)],
        out_specs=pl.BlockSpec((tm, tn), lambda i, j: (i, j)),
        out_shape=jax.ShapeDtypeStruct((n, D_FF), BF16),
        scratch_shapes=[pltpu.VMEM((BF16_ROWS + tm, d), BF16)],
        compiler_params=_cparams(("parallel", "arbitrary")),
        name="ffn_up",
    )(x1b, x1b, w_gate, w_up, conv_w, conv_w, conv_b, conv_b)


def _ffn_down_body(a_ref, w_ref, x_ref, g_ref, b_ref, o_ref, acc_ref, *, alpha):
    k = pl.program_id(1)

    @pl.when(k == 0)
    def _():
        acc_ref[...] = jnp.zeros_like(acc_ref)

    acc_ref[...] += jnp.dot(a_ref[...], w_ref[...], preferred_element_type=F32)

    @pl.when(k == pl.num_programs(1) - 1)
    def _():
        o_ref[...] = _layer_norm(alpha * x_ref[...] + acc_ref[...], g_ref[...], b_ref[...])


def _ffn_down(act, w, x1, g, b, *, alpha, tm, tk):
    n, d = x1.shape
    row = lambda: pl.BlockSpec((tm, d), lambda i, k: (i, 0))
    vec = lambda: pl.BlockSpec((1, d), lambda i, k: (0, 0))
    return pl.pallas_call(
        functools.partial(_ffn_down_body, alpha=alpha),
        grid=(n // tm, D_FF // tk),
        in_specs=[pl.BlockSpec((tm, tk), lambda i, k: (i, k)),
                  pl.BlockSpec((tk, d), lambda i, k: (k, 0)),
                  row(), vec(), vec()],
        out_specs=row(),
        out_shape=jax.ShapeDtypeStruct((n, d), F32),
        scratch_shapes=[pltpu.VMEM((tm, d), F32)],
        compiler_params=_cparams(("parallel", "arbitrary")),
        name="ffn_down_ln",
    )(act, w, x1, g, b)


def _layer(x, depth, w_in, rnn_conv_w, rnn_conv_b, lru_wa, lru_ba, lru_wi, lru_bi, lru_lambda,
           w_out, ln1_g, ln1_b, w_up, ffn_conv_w, ffn_conv_b, w_down, ln2_g, ln2_b):
    bsz, seq_len, d = x.shape
    n = bsz * seq_len
    topk = min(TOPK_MAX, seq_len // 4)
    alpha = (2 * depth) ** 0.25
    tm = min(1024, seq_len)
    x2 = x.reshape(n, d)
    xb = x2.astype(BF16)
    wb = w_in.astype(BF16)
    proj = functools.partial(_proj, xb, tm=tm)

    xr = proj(wb[:, _O_XR:_O_GR], act="none", out_dtype=F32, tn=512, name="proj_xr")
    gg = proj(wb[:, _O_GR:_O_Q], act="gelu", out_dtype=F32, tn=512, name="proj_gelu")
    gates = proj(wb[:, _O_GRNN:_O_END], act="sigmoid", out_dtype=F32, tn=512, name="proj_gates")
    q_tab = _rope_tables(seq_len, HEAD_DIM, HEAD_DIM ** -0.5)
    k_tab = _rope_tables(seq_len, HEAD_DIM, 1.0)
    qkv_tab = jnp.stack([q_tab, k_tab, _identity_tables(seq_len)], axis=0)
    dq, dkv = N_HEADS * HEAD_DIM, N_KV_HEADS * HEAD_DIM
    qkv = proj(wb[:, _O_Q:_O_QI], act="rope", out_dtype=BF16, tn=dkv, name="proj_qkv", tab=qkv_tab,
               kind_bounds=(dq // dkv, dq // dkv + 1), shift=HEAD_DIM // ROPE_FRAC // 2, seq_len=seq_len)
    idx_shift = IDX_DIM // ROPE_FRAC // 2
    qi_tab = _rope_tables(seq_len, IDX_DIM, IDX_DIM ** -0.5)[None]
    qi = proj(wb[:, _O_QI:_O_KI], act="rope", out_dtype=BF16, tn=512, name="proj_qi", tab=qi_tab,
              shift=idx_shift, seq_len=seq_len)
    w_ki = wb[:, _O_KI:_O_WI]
    zeros_ki = jnp.zeros_like(w_ki)
    w_kab = jnp.concatenate([w_ki, zeros_ki, zeros_ki, w_ki], axis=1)
    ki_tab = _rope_tables(seq_len, IDX_DIM, 1.0)[None]
    kab = proj(w_kab, act="rope", out_dtype=BF16, tn=2 * LANES, name="proj_ki", tab=ki_tab,
               shift=idx_shift, seq_len=seq_len)
    w_wi = jnp.concatenate([wb[:, _O_WI:_O_GRNN], jnp.zeros((d, LANES - IDX_HEADS), BF16)], axis=1)
    wi = proj(w_wi, act="scale", out_dtype=F32, tn=LANES, name="proj_wi", scale=IDX_HEADS ** -0.5)

    r3 = lambda a: a.reshape(bsz, seq_len, a.shape[-1])
    yr = _rnn(r3(xr), r3(gg), r3(gates), rnn_conv_w, rnn_conv_b.reshape(1, D_RNN), lru_wa, lru_ba, lru_wi,
              lru_bi, lru_lambda.reshape(1, D_RNN), tt=min(RNN_CHUNK, seq_len))

    merged = _attention(r3(qi), r3(kab), r3(wi), r3(qkv), yr, r3(gates), topk=topk,
                        tq=min(Q_TILE, seq_len), kc=min(KEY_CHUNK, seq_len))

    x1, x1b = _oproj(merged.reshape(n, d), w_out.astype(BF16), x2, ln1_g.reshape(1, d), ln1_b.reshape(1, d),
                     alpha=alpha, tm=min(256, seq_len))

    wub = w_up.astype(BF16)
    act = _ffn_up(x1b, wub, wub, ffn_conv_w, ffn_conv_b.reshape(1, 2 * D_FF), seq_len=seq_len, tm=tm, tn=512)
    out = _ffn_down(act, w_down.astype(BF16), x1, ln2_g.reshape(1, d), ln2_b.reshape(1, d), alpha=alpha,
                    tm=min(512, seq_len), tk=512)
    return out.reshape(bsz, seq_len, d)


def kernel(x, w_in, rnn_conv_w, rnn_conv_b, lru_wa, lru_ba, lru_wi, lru_bi, lru_lambda, w_out, ln1_g, ln1_b,
           w_up, ffn_conv_w, ffn_conv_b, w_down, ln2_g, ln2_b):
    depth = w_in.shape[0]
    for l in range(depth):
        x = _layer(x, depth, w_in[l], rnn_conv_w[l], rnn_conv_b[l], lru_wa[l], lru_ba[l], lru_wi[l], lru_bi[l],
                   lru_lambda[l], w_out[l], ln1_g[l], ln1_b[l], w_up[l], ffn_conv_w[l], ffn_conv_b[l], w_down[l],
                   ln2_g[l], ln2_b[l])
    return x
```
